```python
import jax, jax.numpy as jnp
from jax import lax
import numpy as np

D_MODEL = 1024
BATCH = 8
SEQ = 2048
DEPTH = 4

GRID_W = 64
CTX_LEN = 256
N_MOD = 6
MLA_H = 4
MLA_NOPE = 64
MLA_ROPE = 32
MLA_V = 64
MLA_Q_LORA = 256
MLA_KV_LORA = 128
Q_BLOCK = 128
ROPE_THETA = 10000.0
NA_H = 4
NA_D = 64
NA_W = NA_H * NA_D
NA_ROWS = 8
NA_COLS = 16
DN_H = 4
DN_D = 128
DN_W = DN_H * DN_D
DN_CONV = 5
DN_CHUNK = 64
DN_PROJ = 4 * DN_W + 4 * DN_H
IN_W = MLA_Q_LORA + MLA_KV_LORA + MLA_ROPE + 3 * NA_W + DN_PROJ
MIX_W = MLA_H * MLA_V + NA_W + DN_W
FFN_HIDDEN = -(-8 * D_MODEL // (3 * 256)) * 256
EPS = 1e-6

kernel_name = "hybrid_mla_natten_gdn_diffusion_block"


def rms_norm(x, g):
    xf = x.astype(jnp.float32)
    y = xf * lax.rsqrt(jnp.mean(xf * xf, axis=-1, keepdims=True) + EPS)
    return (y * g.astype(jnp.float32)).astype(x.dtype)


def l2_norm(x):
    xf = x.astype(jnp.float32)
    return xf * lax.rsqrt(jnp.sum(xf * xf, axis=-1, keepdims=True) + EPS)


def modulate(h, shift, scale):
    return h * (1 + scale) + shift


def split_in(p):
    sizes = [MLA_Q_LORA, MLA_KV_LORA, MLA_ROPE, NA_W, NA_W, NA_W, DN_PROJ]
    offs = [int(o) for o in np.cumsum(sizes)[:-1]]
    return jnp.split(p, offs, axis=-1)


def axial_rope_tables(n_tok):
    t = jnp.arange(n_tok)
    pos = jnp.stack([t // GRID_W, t % GRID_W], axis=-1).astype(jnp.float32)
    n_freq = MLA_ROPE // 4
    inv = jnp.power(ROPE_THETA, -jnp.arange(n_freq, dtype=jnp.float32) / n_freq)
    ang = pos[:, :, None] * inv
    return jnp.cos(ang), jnp.sin(ang)


def apply_axial_rope(x, cos, sin):
    B, T, H, R = x.shape
    xa = x.astype(jnp.float32).reshape(B, T, H, 2, 2, R // 4)
    x1, x2 = xa[..., 0, :], xa[..., 1, :]
    c, s = cos[None, :, None], sin[None, :, None]
    y = jnp.stack([x1 * c - x2 * s, x1 * s + x2 * c], axis=-2)
    return y.reshape(B, T, H, R).astype(x.dtype)


def softmax_attention(q, k, v, scale):
    s = jnp.einsum('bqhd,bkhd->bhqk', q, k).astype(jnp.float32) * scale
    p = jax.nn.softmax(s, axis=-1).astype(v.dtype)
    return jnp.einsum('bhqk,bkhd->bqhd', p, v)


def blocked_attention(q, k, v, scale):
    B, T, H, d = q.shape
    nb = T // Q_BLOCK
    qb = q.reshape(B, nb, Q_BLOCK, H, d).transpose(1, 0, 2, 3, 4)
    out = lax.map(lambda blk: softmax_attention(blk, k, v, scale), qb)
    return out.transpose(1, 0, 2, 3, 4).reshape(B, T, H, v.shape[-1])


def mla_mixer(q_c, kv_c, k_pe, cq_c, ckv_c, ck_pe, g_q, g_kv, w_q_up, w_kv_up, cos, sin, need_ctx):
    B, T, _ = q_c.shape
    scale = (MLA_NOPE + MLA_ROPE) ** -0.5

    def q_heads(qc):
        return (rms_norm(qc, g_q) @ w_q_up).reshape(qc.shape[0], qc.shape[1], MLA_H, MLA_NOPE + MLA_ROPE)

    def kv_heads(kvc, pe):
        kv = (rms_norm(kvc, g_kv) @ w_kv_up).reshape(kvc.shape[0], kvc.shape[1], MLA_H, MLA_NOPE + MLA_V)
        pe = jnp.broadcast_to(pe, kv.shape[:3] + (MLA_ROPE,))
        return jnp.concatenate([kv[..., :MLA_NOPE], pe], axis=-1), kv[..., MLA_NOPE:]

    q = q_heads(q_c)
    q = jnp.concatenate([q[..., :MLA_NOPE], apply_axial_rope(q[..., MLA_NOPE:], cos, sin)], axis=-1)
    k, v = kv_heads(kv_c, apply_axial_rope(k_pe[:, :, None, :], cos, sin))
    ck, cv = kv_heads(ckv_c, ck_pe[:, :, None, :])
    keys = jnp.concatenate([ck, k], axis=1)
    vals = jnp.concatenate([cv, v], axis=1)
    out = blocked_attention(q, keys, vals, scale).reshape(B, T, MLA_H * MLA_V)
    if not need_ctx:
        return out, None
    cq = q_heads(cq_c)
    c_out = softmax_attention(cq, ck, cv, scale).reshape(cq.shape[0], cq.shape[1], MLA_H * MLA_V)
    return out, c_out


def na_mixer(q, k, v, cq, ck, cv, rel_bias, need_ctx):
    B, T, _ = q.shape
    rows = T // GRID_W
    kh = min(NA_ROWS, rows)
    heads = lambda t: t.reshape(t.shape[0], t.shape[1], NA_H, NA_D)
    q, k, v, cq, ck, cv = heads(q), heads(k), heads(v), heads(cq), heads(ck), heads(cv)
    grid = lambda t: t.reshape(B, rows, GRID_W, NA_H, NA_D)
    qg, kg, vg = grid(q), grid(k), grid(v)
    r = jnp.arange(rows)
    row_idx = jnp.clip(r - kh // 2, 0, rows - kh)[:, None] + jnp.arange(kh)[None, :]
    kb, vb = kg[:, row_idx], vg[:, row_idx]
    col = jnp.arange(GRID_W)
    col_start = jnp.clip(col - NA_COLS // 2, 0, GRID_W - NA_COLS)
    col_ok = (col[None, :] >= col_start[:, None]) & (col[None, :] < col_start[:, None] + NA_COLS)
    dr_i = row_idx - r[:, None] + NA_ROWS - 1
    dc_i = jnp.clip(col[None, :] - col[:, None] + NA_COLS - 1, 0, 2 * NA_COLS - 2)
    bias = rel_bias[:, dr_i[:, None, :, None], dc_i[None, :, None, :]].astype(jnp.float32)
    scale = NA_D ** -0.5
    s_loc = jnp.einsum('brqhd,brjkhd->bhrqjk', qg, kb).astype(jnp.float32) * scale + bias
    s_loc = jnp.where(col_ok[:, None, :], s_loc, -jnp.inf).reshape(B, NA_H, rows, GRID_W, kh * GRID_W)
    s_ctx = jnp.einsum('brqhd,bchd->bhrqc', qg, ck).astype(jnp.float32) * scale
    p = jax.nn.softmax(jnp.concatenate([s_loc, s_ctx], axis=-1), axis=-1).astype(v.dtype)
    p_loc = p[..., :kh * GRID_W].reshape(B, NA_H, rows, GRID_W, kh, GRID_W)
    p_ctx = p[..., kh * GRID_W:]
    o = (jnp.einsum('bhrqjk,brjkhd->brqhd', p_loc, vb)
         + jnp.einsum('bhrqc,bchd->brqhd', p_ctx, cv)).reshape(B, T, NA_W)
    if not need_ctx:
        return o, None
    c_out = softmax_attention(cq, ck, cv, scale).reshape(cq.shape[0], cq.shape[1], NA_W)
    return o, c_out


def centred_depthwise_conv(x, w):
    K = w.shape[0]
    return lax.conv_general_dilated(x, w[:, None, :], window_strides=(1,), padding=[(K // 2, K // 2)],
                                    dimension_numbers=('NWC', 'WIO', 'NWC'), feature_group_count=x.shape[-1])


def chunk_gated_delta_rule(q, k, v, g, beta, state, with_out):
    f32 = jnp.float32
    B, T, H, dk = q.shape
    dv = v.shape[-1]
    N = T // DN_CHUNK

    def blocks(t):
        t = t.astype(f32).reshape((B, N, DN_CHUNK, H) + t.shape[3:])
        return jnp.moveaxis(t, 3, 1)

    q = blocks(q) * dk ** -0.5
    k, v, g, beta = blocks(k), blocks(v), blocks(g), blocks(beta)
    gc = jnp.cumsum(g, axis=-1)
    idx = jnp.arange(DN_CHUNK)
    lower = idx[:, None] >= idx[None, :]
    strict = idx[:, None] > idx[None, :]
    decay = jnp.exp(jnp.where(lower, gc[..., :, None] - gc[..., None, :], -jnp.inf))
    kb = k * beta[..., None]
    L = jnp.where(strict, jnp.einsum('bhnid,bhnjd->bhnij', kb, k) * decay, 0.0)
    eye = jnp.eye(DN_CHUNK, dtype=f32)
    tinv = lax.linalg.triangular_solve(L + eye, jnp.broadcast_to(eye, L.shape), left_side=True,
                                       lower=True, unit_diagonal=True)
    u = tinv @ (v * beta[..., None])
    w = tinv @ (kb * jnp.exp(gc)[..., None])
    k_tail = k * jnp.exp(gc[..., -1:] - gc)[..., None]
    g_last = jnp.exp(gc[..., -1])
    mv = lambda t: jnp.moveaxis(t, 2, 0)
    xs = (mv(u), mv(w), mv(k_tail), mv(g_last))
    if with_out:
        q_dec = q * jnp.exp(gc)[..., None]
        a_intra = jnp.einsum('bhnid,bhnjd->bhnij', q, k) * decay
        xs = xs + (mv(q_dec), mv(a_intra))

    def step(S, xn):
        v_new = xn[0] - xn[1] @ S
        S_new = S * xn[3][..., None, None] + jnp.swapaxes(xn[2], -1, -2) @ v_new
        if with_out:
            return S_new, xn[4] @ S + xn[5] @ v_new
        return S_new, None

    S, o = lax.scan(step, state.astype(f32), xs)
    if with_out:
        o = o.transpose(1, 0, 3, 2, 4).reshape(B, T, H, dv)
    return o, S


def gated_deltanet_mixer(p, pc, conv_w, a_log, dt_bias, g_out, need_ctx):
    def prep(t):
        B, T, _ = t.shape
        qkv = jax.nn.silu(centred_depthwise_conv(t[..., :3 * DN_W], conv_w))
        q, k, v = [s.reshape(B, T, DN_H, DN_D) for s in jnp.split(qkv, 3, axis=-1)]
        a = t[..., 4 * DN_W:4 * DN_W + 2 * DN_H].reshape(B, T, 2, DN_H).astype(jnp.float32)
        b = t[..., 4 * DN_W + 2 * DN_H:].reshape(B, T, 2, DN_H).astype(jnp.float32)
        g = -jnp.exp(a_log.astype(jnp.float32)) * jax.nn.softplus(a + dt_bias.astype(jnp.float32))
        return l2_norm(q), l2_norm(k), v, g, jax.nn.sigmoid(b)

    def direction(seq, d):
        q, k, v, g, beta = seq
        f = (lambda t: jnp.flip(t, axis=1)) if d == 1 else (lambda t: t)
        return f(q), f(k), f(v), f(g[:, :, d]), f(beta[:, :, d])

    lat, cseq = prep(p), prep(pc)
    B = p.shape[0]
    o_lat, o_ctx = 0.0, 0.0
    for d in range(2):
        s0 = jnp.zeros((B, DN_H, DN_D, DN_D), jnp.float32)
        oc, s_ctx = chunk_gated_delta_rule(*direction(cseq, d), s0, need_ctx)
        ol, _ = chunk_gated_delta_rule(*direction(lat, d), s_ctx, True)
        o_lat = o_lat + (jnp.flip(ol, axis=1) if d == 1 else ol)
        if need_ctx:
            o_ctx = o_ctx + (jnp.flip(oc, axis=1) if d == 1 else oc)

    def out_gate(o, t):
        z = t[..., 3 * DN_W:4 * DN_W].reshape(o.shape).astype(jnp.float32)
        return (rms_norm(o, g_out) * jax.nn.silu(z)).reshape(o.shape[0], o.shape[1], DN_W).astype(t.dtype)

    return out_gate(o_lat, p), (out_gate(o_ctx, pc) if need_ctx else None)


def swiglu(h, w_gate, w_up, w_down):
    return (jax.nn.silu(h @ w_gate) * (h @ w_up)) @ w_down


def setup_inputs(seed: int = 0) -> dict:
    key = jax.random.key(seed)
    ks = jax.random.split(key, 24)
    nrm = lambda k, shape, s: jax.random.normal(k, shape, jnp.float32) * s
    gain = lambda k, shape: 1.0 + 0.1 * jax.random.normal(k, shape, jnp.float32)
    dt = jnp.exp(jax.random.uniform(ks[15], (DEPTH, 2, DN_H), jnp.float32, np.log(1e-3), np.log(1e-1)))
    return {
        "x": nrm(ks[0], (BATCH, SEQ, D_MODEL), 1.0),
        "c": nrm(ks[1], (BATCH, D_MODEL), 1.0),
        "ctx": nrm(ks[2], (BATCH, CTX_LEN, D_MODEL), 1.0),
        "c_ctx": nrm(ks[3], (D_MODEL,), 1.0),
        "w_ada": nrm(ks[4], (DEPTH, D_MODEL, N_MOD * D_MODEL), 0.5 * D_MODEL ** -0.5),
        "b_ada": nrm(ks[5], (DEPTH, N_MOD * D_MODEL), 0.01),
        "g_mix": gain(ks[6], (DEPTH, D_MODEL)),
        "w_in": nrm(ks[7], (DEPTH, D_MODEL, IN_W), D_MODEL ** -0.5),
        "mla_g_q": gain(ks[8], (DEPTH, MLA_Q_LORA)),
        "mla_g_kv": gain(ks[9], (DEPTH, MLA_KV_LORA)),
        "mla_w_q_up": nrm(ks[10], (DEPTH, MLA_Q_LORA, MLA_H * (MLA_NOPE + MLA_ROPE)), MLA_Q_LORA ** -0.5),
        "mla_w_kv_up": nrm(ks[11], (DEPTH, MLA_KV_LORA, MLA_H * (MLA_NOPE + MLA_V)), MLA_KV_LORA ** -0.5),
        "na_rel_bias": nrm(ks[12], (DEPTH, NA_H, 2 * NA_ROWS - 1, 2 * NA_COLS - 1), 0.1),
        "dn_conv_w": nrm(ks[13], (DEPTH, DN_CONV, 3 * DN_W), DN_CONV ** -0.5),
        "dn_a_log": jnp.log(jax.random.uniform(ks[14], (DEPTH, 2, DN_H), jnp.float32, 1.0, 16.0)),
        "dn_dt_bias": jnp.log(jnp.expm1(dt)),
        "dn_g_out": gain(ks[16], (DEPTH, DN_D)),
        "w_out": nrm(ks[17], (DEPTH, MIX_W, D_MODEL), MIX_W ** -0.5),
        "g_ffn": gain(ks[18], (DEPTH, D_MODEL)),
        "w_gate": nrm(ks[19], (DEPTH, D_MODEL, FFN_HIDDEN), D_MODEL ** -0.5),
        "w_up": nrm(ks[20], (DEPTH, D_MODEL, FFN_HIDDEN), D_MODEL ** -0.5),
        "w_down": nrm(ks[21], (DEPTH, FFN_HIDDEN, D_MODEL), FFN_HIDDEN ** -0.5),
        "g_final": gain(ks[22], (D_MODEL,)),
    }


def reference(x, c, ctx, c_ctx, w_ada, b_ada, g_mix, w_in, mla_g_q, mla_g_kv, mla_w_q_up, mla_w_kv_up,
              na_rel_bias, dn_conv_w, dn_a_log, dn_dt_bias, dn_g_out, w_out, g_ffn, w_gate, w_up, w_down,
              g_final):
    T = x.shape[1]
    cos, sin = axial_rope_tables(T)
    sc = jax.nn.silu(c)
    scc = jax.nn.silu(c_ctx)[None]
    for l in range(DEPTH):
        need_ctx = l < DEPTH - 1
        mx = (sc @ w_ada[l] + b_ada[l]).reshape(-1, N_MOD, 1, D_MODEL)
        mc = (scc @ w_ada[l] + b_ada[l]).reshape(1, N_MOD, 1, D_MODEL)
        h = modulate(rms_norm(x, g_mix[l]), mx[:, 0], mx[:, 1])
        hc = modulate(rms_norm(ctx, g_mix[l]), mc[:, 0], mc[:, 1])
        mq, mkv, mpe, nq, nk, nv, dn = split_in(h @ w_in[l])
        cmq, cmkv, cmpe, cnq, cnk, cnv, cdn = split_in(hc @ w_in[l])
        mla_o, mla_c = mla_mixer(mq, mkv, mpe, cmq, cmkv, cmpe, mla_g_q[l], mla_g_kv[l], mla_w_q_up[l],
                                 mla_w_kv_up[l], cos, sin, need_ctx)
        na_o, na_c = na_mixer(nq, nk, nv, cnq, cnk, cnv, na_rel_bias[l], need_ctx)
        dn_o, dn_c = gated_deltanet_mixer(dn, cdn, dn_conv_w[l], dn_a_log[l], dn_dt_bias[l], dn_g_out[l], need_ctx)
        x = x + mx[:, 2] * (jnp.concatenate([mla_o, na_o, dn_o], axis=-1) @ w_out[l])
        h = modulate(rms_norm(x, g_ffn[l]), mx[:, 3], mx[:, 4])
        x = x + mx[:, 5] * swiglu(h, w_gate[l], w_up[l], w_down[l])
        if need_ctx:
            ctx = ctx + mc[:, 2] * (jnp.concatenate([mla_c, na_c, dn_c], axis=-1) @ w_out[l])
            hc = modulate(rms_norm(ctx, g_ffn[l]), mc[:, 3], mc[:, 4])
            ctx = ctx + mc[:, 5] * swiglu(hc, w_gate[l], w_up[l], w_down[l])
    return rms_norm(x, g_final)
```

```python
import functools

import numpy as np
import jax
import jax.numpy as jnp
from jax import lax
from jax.experimental import pallas as pl
from jax.experimental.pallas import tpu as pltpu

D_MODEL = 1024
DEPTH = 4
GRID_W = 64
CTX_LEN = 256
N_MOD = 6
MLA_H = 4
MLA_NOPE = 64
MLA_ROPE = 32
MLA_V = 64
MLA_Q_LORA = 256
MLA_KV_LORA = 128
ROPE_THETA = 10000.0
NA_H = 4
NA_D = 64
NA_W = NA_H * NA_D
NA_ROWS = 8
NA_COLS = 16
DN_H = 4
DN_D = 128
DN_W = DN_H * DN_D
DN_CONV = 5
DN_CHUNK = 64
FFN_HIDDEN = -(-8 * D_MODEL // (3 * 256)) * 256
EPS = 1e-6

LANES = 128
MLA_HP = 128
MLA_W = MLA_H * MLA_HP
C_MQ = 0
C_MKV = C_MQ + MLA_Q_LORA
C_PE = C_MKV + MLA_KV_LORA
C_PEP = C_PE + LANES
C_NA = C_PEP + LANES
C_DN = C_NA + 3 * NA_W
C_AB = C_DN + 4 * DN_W
IN_WP = C_AB + LANES

NA_QROWS = 4
NA_KROWS = NA_QROWS + NA_ROWS - 1
NA_TQ = NA_QROWS * GRID_W
NA_TK = NA_KROWS * GRID_W
NEG = -1e30

VMEM_LIMIT = 56 * 1024 * 1024

F32 = jnp.float32
BF16 = jnp.bfloat16


def _cparams(sem):
    return pltpu.CompilerParams(dimension_semantics=sem, vmem_limit_bytes=VMEM_LIMIT)


def _mm(a, b):
    return jnp.dot(a.astype(BF16), b.astype(BF16), preferred_element_type=F32)


def _mm_nt(a, b):
    return lax.dot_general(a.astype(BF16), b.astype(BF16), (((1,), (1,)), ((), ())),
                           preferred_element_type=F32)


def _split(a):
    hi = a.astype(BF16)
    return hi, (a - hi.astype(F32)).astype(BF16)


def _mm3(a, b):
    ah, al = _split(a)
    bh, bl = _split(b)
    d = functools.partial(jnp.dot, preferred_element_type=F32)
    return d(ah, bh) + (d(ah, bl) + d(al, bh))


def _mm3_nt(a, b):
    ah, al = _split(a)
    bh, bl = _split(b)
    d = functools.partial(lax.dot_general, dimension_numbers=(((1,), (1,)), ((), ())),
                          preferred_element_type=F32)
    return d(ah, bh) + (d(ah, bl) + d(al, bh))


def _sigmoid(x):
    return 1.0 / (1.0 + jnp.exp(-x))


def _silu(x):
    return x * _sigmoid(x)


def _softplus(x):
    return jnp.maximum(x, 0.0) + jnp.log(1.0 + jnp.exp(-jnp.abs(x)))


def _rms(x, g):
    return x * lax.rsqrt(jnp.mean(x * x, axis=-1, keepdims=True) + EPS) * g


def _row_is_ctx(tile_rows, tile_idx):
    rows = tile_idx * tile_rows + lax.broadcasted_iota(jnp.int32, (tile_rows, 1), 0)
    return rows < CTX_LEN


def _ada_kernel(c_ref, w_ref, b_ref, o_ref):
    c = c_ref[...]
    o_ref[...] = _mm3(_silu(c), w_ref[...]) + b_ref[...]


CTX_ROW = 8


def _cond_rows(c, c_ctx):
    B = c.shape[0]
    return jnp.concatenate([c, jnp.zeros((CTX_ROW - B, D_MODEL), F32), c_ctx[None],
                            jnp.zeros((16 - CTX_ROW - 1, D_MODEL), F32)], axis=0)


def _ada_call(c_all, w_ada, b_ada):
    tn = 1536
    n_out = N_MOD * D_MODEL
    return pl.pallas_call(
        _ada_kernel,
        grid=(DEPTH, n_out // tn),
        in_specs=[
            pl.BlockSpec((16, D_MODEL), lambda l, j: (0, 0)),
            pl.BlockSpec((None, D_MODEL, tn), lambda l, j: (l, 0, j)),
            pl.BlockSpec((None, 1, tn), lambda l, j: (l, 0, j)),
        ],
        out_specs=pl.BlockSpec((None, 16, tn), lambda l, j: (l, 0, j)),
        out_shape=jax.ShapeDtypeStruct((DEPTH, 16, n_out), F32),
        compiler_params=_cparams(("parallel", "parallel")),
        name="adaln",
    )(c_all, w_ada, b_ada.reshape(DEPTH, 1, n_out))


def _in_kernel(x_ref, mb_ref, mc_ref, g_ref, w_ref, gq_ref, gkv_ref, wq_ref, wkv_ref,
               tqc_ref, tqs_ref, tkc_ref, tks_ref,
               q_ref, k_ref, v_ref, na_ref, dn_ref, ab_ref, *, tm):
    i = pl.program_id(1)
    is_ctx = _row_is_ctx(tm, i)
    mb = mb_ref[...]
    mc = mc_ref[...]
    shift = jnp.where(is_ctx, mc[:, 0:D_MODEL], mb[:, 0:D_MODEL])
    scale = jnp.where(is_ctx, mc[:, D_MODEL:2 * D_MODEL], mb[:, D_MODEL:2 * D_MODEL])
    h = (_rms(x_ref[...], g_ref[...]) * (1.0 + scale) + shift).astype(BF16)

    pm = jnp.dot(h, w_ref[:, C_MQ:C_NA], preferred_element_type=F32)
    qn = _rms(pm[:, C_MQ:C_MKV], gq_ref[...])
    qq = _mm(qn, wq_ref[...])
    kvn = _rms(pm[:, C_MKV:C_PE], gkv_ref[...])
    kk = _mm(kvn, wkv_ref[...])
    pe = pm[:, C_PE:C_PEP] * tkc_ref[...] + pm[:, C_PEP:C_NA] * tks_ref[...]
    tqc = tqc_ref[...]
    tqs = tqs_ref[...]
    for hh in range(MLA_H):
        sl = slice(hh * MLA_HP, (hh + 1) * MLA_HP)
        sl2 = slice(MLA_W + hh * MLA_HP, MLA_W + (hh + 1) * MLA_HP)
        q_ref[:, sl] = (qq[:, sl] * tqc + qq[:, sl2] * tqs).astype(BF16)
        k_ref[:, sl] = (kk[:, sl] + pe).astype(BF16)
    v_ref[...] = kk[:, MLA_W:].astype(BF16)

    na_ref[...] = jnp.dot(h, w_ref[:, C_NA:C_DN], preferred_element_type=F32).astype(BF16)
    dn_ref[...] = jnp.dot(h, w_ref[:, C_DN:C_AB], preferred_element_type=F32).astype(BF16)
    ab_ref[...] = jnp.dot(h, w_ref[:, C_AB:IN_WP], preferred_element_type=F32)


def _in_call(l, xs, mod4, g_mix, w_in_p, g_q, g_kv, wq_p, wkv_p, tabs):
    B, S, _ = xs.shape
    tm = 768
    row = lambda b, i: (b, i, 0)
    tab = lambda b, i: (i, 0)
    const2 = lambda b, i: (l, 0, 0)
    return pl.pallas_call(
        functools.partial(_in_kernel, tm=tm),
        grid=(B, S // tm),
        in_specs=[
            pl.BlockSpec((None, tm, D_MODEL), row),
            pl.BlockSpec((None, None, 1, N_MOD * D_MODEL), lambda b, i: (l, b, 0, 0)),
            pl.BlockSpec((None, None, 1, N_MOD * D_MODEL), lambda b, i: (l, CTX_ROW, 0, 0)),
            pl.BlockSpec((None, 1, D_MODEL), const2),
            pl.BlockSpec((None, D_MODEL, IN_WP), const2),
            pl.BlockSpec((None, 1, MLA_Q_LORA), const2),
            pl.BlockSpec((None, 1, MLA_KV_LORA), const2),
            pl.BlockSpec((None, MLA_Q_LORA, 2 * MLA_W), const2),
            pl.BlockSpec((None, MLA_KV_LORA, MLA_W + MLA_H * MLA_V), const2),
            pl.BlockSpec((tm, MLA_HP), tab),
            pl.BlockSpec((tm, MLA_HP), tab),
            pl.BlockSpec((tm, MLA_HP), tab),
            pl.BlockSpec((tm, MLA_HP), tab),
        ],
        out_specs=[
            pl.BlockSpec((None, tm, MLA_W), row),
            pl.BlockSpec((None, tm, MLA_W), row),
            pl.BlockSpec((None, tm, MLA_H * MLA_V), row),
            pl.BlockSpec((None, tm, 3 * NA_W), row),
            pl.BlockSpec((None, tm, 4 * DN_W), row),
            pl.BlockSpec((None, tm, LANES), row),
        ],
        out_shape=[
            jax.ShapeDtypeStruct((B, S, MLA_W), BF16),
            jax.ShapeDtypeStruct((B, S, MLA_W), BF16),
            jax.ShapeDtypeStruct((B, S, MLA_H * MLA_V), BF16),
            jax.ShapeDtypeStruct((B, S, 3 * NA_W), BF16),
            jax.ShapeDtypeStruct((B, S, 4 * DN_W), BF16),
            jax.ShapeDtypeStruct((B, S, LANES), F32),
        ],
        compiler_params=_cparams(("parallel", "parallel")),
        name=f"mix_in_{l}",
    )(xs, mod4, mod4, g_mix, w_in_p, g_q, g_kv, wq_p, wkv_p, *tabs)


def _mla_kernel(q_ref, k_ref, v_ref, o_ref):
    i = pl.program_id(1)

    def attend(nk):
        for hh in range(MLA_H):
            sl = slice(hh * MLA_HP, (hh + 1) * MLA_HP)
            s = lax.dot_general(q_ref[:, sl], k_ref[0:nk, sl], (((1,), (1,)), ((), ())),
                                preferred_element_type=F32)
            m = jnp.max(s, axis=-1, keepdims=True)
            p = jnp.exp(s - m)
            den = jnp.sum(p, axis=-1, keepdims=True)
            o = jnp.dot(p.astype(BF16), v_ref[0:nk, hh * MLA_V:(hh + 1) * MLA_V],
                        preferred_element_type=F32)
            o_ref[:, hh * MLA_V:(hh + 1) * MLA_V] = (o / den).astype(BF16)

    @pl.when(i == 0)
    def _():
        attend(CTX_LEN)

    @pl.when(i > 0)
    def _():
        attend(k_ref.shape[0])


def _mla_call(q, k, v):
    B, S, _ = q.shape
    tq = CTX_LEN
    return pl.pallas_call(
        _mla_kernel,
        grid=(B, S // tq),
        in_specs=[
            pl.BlockSpec((None, tq, MLA_W), lambda b, i: (b, i, 0)),
            pl.BlockSpec((None, S, MLA_W), lambda b, i: (b, 0, 0)),
            pl.BlockSpec((None, S, MLA_H * MLA_V), lambda b, i: (b, 0, 0)),
        ],
        out_specs=pl.BlockSpec((None, tq, MLA_H * MLA_V), lambda b, i: (b, i, 0)),
        out_shape=jax.ShapeDtypeStruct((B, S, MLA_H * MLA_V), BF16),
        compiler_params=_cparams(("parallel", "parallel")),
        name="mla_attn",
    )(q, k, v)


def _na_window_start(rb):
    return jnp.clip(NA_QROWS * rb - NA_ROWS // 2, 0, 32 - NA_KROWS)


def _na_kernel(qt_ref, kv_ref, bias_ref, o_ref):
    i = pl.program_id(1)
    scale = NA_D ** -0.5

    def head_slices(hh):
        return (slice(hh * NA_D, (hh + 1) * NA_D), slice(NA_W + hh * NA_D, NA_W + (hh + 1) * NA_D),
                slice(2 * NA_W + hh * NA_D, 2 * NA_W + (hh + 1) * NA_D))

    @pl.when(i == 0)
    def _():
        for hh in range(NA_H):
            qs, ks, vs = head_slices(hh)
            q = (qt_ref[:, qs].astype(F32) * scale).astype(BF16)
            s = lax.dot_general(q, kv_ref[0:CTX_LEN, ks], (((1,), (1,)), ((), ())),
                                preferred_element_type=F32)
            m = jnp.max(s, axis=-1, keepdims=True)
            p = jnp.exp(s - m)
            den = jnp.sum(p, axis=-1, keepdims=True)
            o = jnp.dot(p.astype(BF16), kv_ref[0:CTX_LEN, vs], preferred_element_type=F32)
            o_ref[:, qs] = (o / den).astype(BF16)

    @pl.when(i > 0)
    def _():
        start = pl.multiple_of(CTX_LEN + GRID_W * _na_window_start(i - 1), GRID_W)
        for hh in range(NA_H):
            qs, ks, vs = head_slices(hh)
            q = (qt_ref[:, qs].astype(F32) * scale).astype(BF16)
            s_loc = lax.dot_general(q, kv_ref[pl.ds(start, NA_TK), ks], (((1,), (1,)), ((), ())),
                                    preferred_element_type=F32) + bias_ref[hh]
            s_ctx = lax.dot_general(q, kv_ref[0:CTX_LEN, ks], (((1,), (1,)), ((), ())),
                                    preferred_element_type=F32)
            m = jnp.maximum(jnp.max(s_loc, axis=-1, keepdims=True), jnp.max(s_ctx, axis=-1, keepdims=True))
            p_loc = jnp.exp(s_loc - m)
            p_ctx = jnp.exp(s_ctx - m)
            den = jnp.sum(p_loc, axis=-1, keepdims=True) + jnp.sum(p_ctx, axis=-1, keepdims=True)
            o = (jnp.dot(p_loc.astype(BF16), kv_ref[pl.ds(start, NA_TK), vs], preferred_element_type=F32)
                 + jnp.dot(p_ctx.astype(BF16), kv_ref[0:CTX_LEN, vs], preferred_element_type=F32))
            o_ref[:, qs] = (o / den).astype(BF16)


def _na_bias_class(i):
    rb = jnp.maximum(i - 1, 0)
    return jnp.where(rb == 0, 0, jnp.where(rb == 32 // NA_QROWS - 1, 2, 1))


def _na_call(l, qkv, bias):
    B, S, _ = qkv.shape
    return pl.pallas_call(
        _na_kernel,
        grid=(B, S // NA_TQ),
        in_specs=[
            pl.BlockSpec((None, NA_TQ, 3 * NA_W), lambda b, i: (b, i, 0)),
            pl.BlockSpec((None, S, 3 * NA_W), lambda b, i: (b, 0, 0)),
            pl.BlockSpec((None, NA_H, None, NA_TQ, NA_TK), lambda b, i: (l, 0, _na_bias_class(i), 0, 0)),
        ],
        out_specs=pl.BlockSpec((None, NA_TQ, NA_W), lambda b, i: (b, i, 0)),
        out_shape=jax.ShapeDtypeStruct((B, S, NA_W), BF16),
        compiler_params=_cparams(("parallel", "parallel")),
        name="na_attn",
    )(qkv, qkv, bias)


def _na_bias_tables(rel_bias):
    q_off = np.array([0, NA_ROWS // 2, NA_KROWS - NA_QROWS])
    t = np.arange(NA_QROWS)
    kr = np.arange(NA_KROWS)
    qr = q_off[:, None] + t[None, :]
    first = np.stack([np.zeros(NA_QROWS, int), t, np.full(NA_QROWS, NA_KROWS - NA_ROWS)])
    row_ok = (kr[None, None, :] >= first[:, :, None]) & (kr[None, None, :] < first[:, :, None] + NA_ROWS)
    dr = np.clip(kr[None, None, :] - qr[:, :, None] + NA_ROWS - 1, 0, 2 * NA_ROWS - 2)
    col = np.arange(GRID_W)
    cs = np.clip(col - NA_COLS // 2, 0, GRID_W - NA_COLS)
    col_ok = (col[None, :] >= cs[:, None]) & (col[None, :] < cs[:, None] + NA_COLS)
    dc = np.clip(col[None, :] - col[:, None] + NA_COLS - 1, 0, 2 * NA_COLS - 2)
    toe = rel_bias[:, :, :, dc]
    tab = toe[:, :, dr]
    ok = row_ok[:, :, :, None, None] & col_ok[None, None, None]
    tab = jnp.where(ok[None, None], tab, NEG)
    tab = tab.transpose(0, 1, 2, 3, 5, 4, 6)
    return tab.reshape(DEPTH, NA_H, 3, NA_TQ, NA_TK).astype(F32)


def _dn_kernel(gp_ref, q_ref, k_ref, v_ref, z_ref, ab_ref, cwq_ref, cwk_ref, cwv_ref, gout_ref,
               o_ref, pad_ref, qs_ref, ks_ref, vs_ref, of_ref, ob_ref, st_ref, *, layer, n_tok):
    h = pl.program_id(1)
    S = n_tok
    T = S - CTX_LEN
    n_chunks = S // DN_CHUNK
    n_ctx_chunks = CTX_LEN // DN_CHUNK
    C = DN_CHUNK
    halo = 8
    lat0 = CTX_LEN + 2 * halo

    zeros_halo = jnp.zeros((halo, DN_D), F32)
    pad_ref[0:halo, :] = zeros_halo
    pad_ref[halo + CTX_LEN:lat0, :] = zeros_halo
    pad_ref[lat0 + T:lat0 + T + halo, :] = zeros_halo

    def conv_act(src_ref, w_ref, dst_ref, normalise, out_scale):
        pad_ref[halo:halo + CTX_LEN, :] = src_ref[0:CTX_LEN, :].astype(F32)
        pad_ref[lat0:lat0 + T, :] = src_ref[CTX_LEN:S, :].astype(F32)
        for base, n, dst0 in ((halo, CTX_LEN, 0), (lat0, T, CTX_LEN)):
            acc = None
            for j in range(DN_CONV):
                off = base + j - DN_CONV // 2
                term = pad_ref[off:off + n, :] * w_ref[j:j + 1, :]
                acc = term if acc is None else acc + term
            y = _silu(acc)
            if normalise:
                y = y * lax.rsqrt(jnp.sum(y * y, axis=-1, keepdims=True) + EPS)
            if out_scale != 1.0:
                y = y * out_scale
            dst_ref[dst0:dst0 + n, :] = y

    conv_act(q_ref, cwq_ref, qs_ref, True, DN_D ** -0.5)
    conv_act(k_ref, cwk_ref, ks_ref, True, 1.0)
    conv_act(v_ref, cwv_ref, vs_ref, False, 1.0)

    st_ref[...] = jnp.zeros(st_ref.shape, F32)
    ii = lax.broadcasted_iota(jnp.int32, (C, C), 0)
    jj = lax.broadcasted_iota(jnp.int32, (C, C), 1)
    eye = ii == jj
    eye_f = jnp.where(eye, 1.0, 0.0)
    incl = (ii >= jj, ii <= jj)
    strict = (ii > jj, ii < jj)
    neg_ea = [-jnp.exp(jnp.full((1, 1), gp_ref[layer * 4 * DN_H + d * DN_H + h], F32)) for d in range(2)]
    dtb = [jnp.full((1, 1), gp_ref[layer * 4 * DN_H + 2 * DN_H + d * DN_H + h], F32) for d in range(2)]

    def chunk_step(n, carry):
        for d in range(2):
            c = n if d == 0 else jnp.where(n < n_ctx_chunks, n_ctx_chunks - 1 - n,
                                           n_chunks + n_ctx_chunks - 1 - n)
            r0 = pl.multiple_of(c * C, C)
            q = qs_ref[pl.ds(r0, C), :]
            k = ks_ref[pl.ds(r0, C), :]
            v = vs_ref[pl.ds(r0, C), :]
            ab = ab_ref[c]
            g_row = neg_ea[d] * _softplus(ab[d:d + 1, :] + dtb[d])
            beta_row = _sigmoid(ab[2 + d:3 + d, :])
            gc_col = jnp.sum(jnp.where(incl[d], g_row, 0.0), axis=1, keepdims=True)
            gc_row = jnp.sum(jnp.where(eye, gc_col, 0.0), axis=0, keepdims=True)
            beta_col = jnp.sum(jnp.where(eye, beta_row, 0.0), axis=1, keepdims=True)
            g_tot = jnp.sum(g_row, axis=1, keepdims=True)
            decay = jnp.exp(jnp.where(incl[d], gc_col - gc_row, NEG))
            e_col = jnp.exp(gc_col)
            kb = k * beta_col
            kq = _mm_nt(jnp.concatenate([kb, q], axis=0), k)
            lmat = jnp.where(strict[d], kq[0:C] * decay, 0.0)
            a_intra = kq[C:2 * C] * decay
            tinv = eye_f - lmat
            pw = lmat
            for _ in range(5):
                pw = _mm3(pw, pw)
                tinv = tinv + _mm3(tinv, pw)
            uw = _mm3(tinv, jnp.concatenate([v * beta_col, kb * e_col], axis=1))
            u = uw[:, 0:DN_D]
            w = uw[:, DN_D:2 * DN_D]
            k_tail = k * jnp.exp(g_tot - gc_col)
            state = st_ref[d]
            ws = _mm(jnp.concatenate([w, q * e_col], axis=0), state)
            v_new = u - ws[0:C]
            o = ws[C:2 * C] + _mm(a_intra, v_new)
            st_ref[d] = state * jnp.exp(g_tot) + _mm(k_tail.T, v_new)
            if d == 0:
                of_ref[pl.ds(r0, C), :] = o
            else:
                ob_ref[pl.ds(r0, C), :] = o
        return carry

    lax.fori_loop(0, n_chunks, chunk_step, 0)

    o = of_ref[...] + ob_ref[...]
    z = z_ref[...].astype(F32)
    o_ref[...] = (_rms(o, gout_ref[...]) * _silu(z)).astype(BF16)


def _dn_call(l, dn, ab_rows, gate_params, conv_w, g_out):
    B, S, _ = dn.shape
    n_chunks = S // DN_CHUNK
    col = lambda off: (lambda b, h: (b, 0, off + h))
    cw = lambda off: (lambda b, h: (l, 0, off + h))
    seq = lambda: pltpu.VMEM((S, DN_D), F32)
    return pl.pallas_call(
        functools.partial(_dn_kernel, layer=l, n_tok=S),
        grid=(B, DN_H),
        in_specs=[
            pl.BlockSpec(memory_space=pltpu.SMEM),
            pl.BlockSpec((None, S, DN_D), col(0)),
            pl.BlockSpec((None, S, DN_D), col(DN_H)),
            pl.BlockSpec((None, S, DN_D), col(2 * DN_H)),
            pl.BlockSpec((None, S, DN_D), col(3 * DN_H)),
            pl.BlockSpec((None, None, n_chunks, 4, DN_CHUNK), lambda b, h: (b, h, 0, 0, 0)),
            pl.BlockSpec((None, DN_CONV, DN_D), cw(0)),
            pl.BlockSpec((None, DN_CONV, DN_D), cw(DN_H)),
            pl.BlockSpec((None, DN_CONV, DN_D), cw(2 * DN_H)),
            pl.BlockSpec((None, 1, DN_D), lambda b, h: (l, 0, 0)),
        ],
        out_specs=pl.BlockSpec((None, S, DN_D), lambda b, h: (b, 0, h)),
        out_shape=jax.ShapeDtypeStruct((B, S, DN_W), BF16),
        scratch_shapes=[
            pltpu.VMEM((S + 24, DN_D), F32),
            seq(), seq(), seq(), seq(), seq(),
            pltpu.VMEM((2, DN_D, DN_D), F32),
        ],
        compiler_params=_cparams(("parallel", "parallel")),
        name="deltanet",
    )(gate_params, dn, dn, dn, dn, ab_rows, conv_w, conv_w, conv_w, g_out)


def _out_kernel(x_ref, mla_ref, na_ref, dn_ref, w_ref, mb_ref, mc_ref, g_ref, xo_ref, h_ref, *, tm):
    i = pl.program_id(1)
    is_ctx = _row_is_ctx(tm, i)
    mb = mb_ref[...]
    mc = mc_ref[...]
    sel = lambda j: jnp.where(is_ctx, mc[:, j * D_MODEL:(j + 1) * D_MODEL], mb[:, j * D_MODEL:(j + 1) * D_MODEL])
    w0 = MLA_H * MLA_V
    w1 = w0 + NA_W
    y = (jnp.dot(mla_ref[...], w_ref[0:w0, :], preferred_element_type=F32)
         + jnp.dot(na_ref[...], w_ref[w0:w1, :], preferred_element_type=F32)
         + jnp.dot(dn_ref[...], w_ref[w1:, :], preferred_element_type=F32))
    x = x_ref[...] + sel(2) * y
    xo_ref[...] = x
    h_ref[...] = (_rms(x, g_ref[...]) * (1.0 + sel(4)) + sel(3)).astype(BF16)


def _out_call(l, xs, mla_o, na_o, dn_o, w_out, mod4, g_ffn):
    B, S, _ = xs.shape
    tm = 768
    row = lambda b, i: (b, i, 0)
    const2 = lambda b, i: (l, 0, 0)
    return pl.pallas_call(
        functools.partial(_out_kernel, tm=tm),
        grid=(B, S // tm),
        in_specs=[
            pl.BlockSpec((None, tm, D_MODEL), row),
            pl.BlockSpec((None, tm, MLA_H * MLA_V), row),
            pl.BlockSpec((None, tm, NA_W), row),
            pl.BlockSpec((None, tm, DN_W), row),
            pl.BlockSpec((None, D_MODEL, D_MODEL), const2),
            pl.BlockSpec((None, None, 1, N_MOD * D_MODEL), lambda b, i: (l, b, 0, 0)),
            pl.BlockSpec((None, None, 1, N_MOD * D_MODEL), lambda b, i: (l, CTX_ROW, 0, 0)),
            pl.BlockSpec((None, 1, D_MODEL), const2),
        ],
        out_specs=[pl.BlockSpec((None, tm, D_MODEL), row), pl.BlockSpec((None, tm, D_MODEL), row)],
        out_shape=[jax.ShapeDtypeStruct((B, S, D_MODEL), F32), jax.ShapeDtypeStruct((B, S, D_MODEL), BF16)],
        compiler_params=_cparams(("parallel", "parallel")),
        name=f"mix_out_{l}",
    )(xs, mla_o, na_o, dn_o, w_out, mod4, mod4, g_ffn)


def _ffn_kernel(x_ref, h_ref, wg_ref, wu_ref, wd_ref, mb_ref, mc_ref, o_ref, *, tm):
    i = pl.program_id(1)
    is_ctx = _row_is_ctx(tm, i)
    gate = jnp.where(is_ctx, mc_ref[:, 5 * D_MODEL:], mb_ref[:, 5 * D_MODEL:])
    h = h_ref[...]
    a = jnp.dot(h, wg_ref[...], preferred_element_type=F32)
    u = jnp.dot(h, wu_ref[...], preferred_element_type=F32)
    hid = (_silu(a) * u).astype(BF16)
    o_ref[...] = x_ref[...] + gate * jnp.dot(hid, wd_ref[...], preferred_element_type=F32)


def _ffn_call(l, xs, h2, w_gate, w_up, w_down, mod4):
    B, S, _ = xs.shape
    tm = 384
    row = lambda b, i: (b, i, 0)
    const2 = lambda b, i: (l, 0, 0)
    resident = dict(pipeline_mode=pl.Buffered(1))
    return pl.pallas_call(
        functools.partial(_ffn_kernel, tm=tm),
        grid=(B, S // tm),
        in_specs=[
            pl.BlockSpec((None, tm, D_MODEL), row),
            pl.BlockSpec((None, tm, D_MODEL), row),
            pl.BlockSpec((None, D_MODEL, FFN_HIDDEN), const2, **resident),
            pl.BlockSpec((None, D_MODEL, FFN_HIDDEN), const2, **resident),
            pl.BlockSpec((None, FFN_HIDDEN, D_MODEL), const2, **resident),
            pl.BlockSpec((None, None, 1, N_MOD * D_MODEL), lambda b, i: (l, b, 0, 0)),
            pl.BlockSpec((None, None, 1, N_MOD * D_MODEL), lambda b, i: (l, CTX_ROW, 0, 0)),
        ],
        out_specs=pl.BlockSpec((None, tm, D_MODEL), row),
        out_shape=jax.ShapeDtypeStruct((B, S, D_MODEL), F32),
        compiler_params=_cparams(("parallel", "parallel")),
        name=f"ffn_{l}",
    )(xs, h2, w_gate, w_up, w_down, mod4, mod4)


def _final_kernel(x_ref, g_ref, o_ref):
    o_ref[...] = _rms(x_ref[...], g_ref[...])


def _final_call(xs, g_final):
    B, S, _ = xs.shape
    tm = CTX_LEN
    T = S - CTX_LEN
    return pl.pallas_call(
        _final_kernel,
        grid=(B, T // tm),
        in_specs=[
            pl.BlockSpec((None, tm, D_MODEL), lambda b, i: (b, i + 1, 0)),
            pl.BlockSpec((1, D_MODEL), lambda b, i: (0, 0)),
        ],
        out_specs=pl.BlockSpec((None, tm, D_MODEL), lambda b, i: (b, i, 0)),
        out_shape=jax.ShapeDtypeStruct((B, T, D_MODEL), F32),
        compiler_params=_cparams(("parallel", "parallel")),
        name="final_norm",
    )(xs, g_final.reshape(1, D_MODEL))


def _rot_half_cols(w):
    e = MLA_ROPE // 4
    return jnp.concatenate([-w[..., e:2 * e], w[..., 0:e], -w[..., 3 * e:4 * e], w[..., 2 * e:3 * e]], axis=-1)


def _layout_w_in(w_in):
    sizes = [MLA_Q_LORA, MLA_KV_LORA, MLA_ROPE, NA_W, NA_W, NA_W, 4 * DN_W, 4 * DN_H]
    offs = np.cumsum([0] + sizes)
    mq, mkv, mpe, nq, nk, nv, dn, ab = [w_in[..., offs[j]:offs[j + 1]] for j in range(len(sizes))]
    z = lambda n: jnp.zeros(w_in.shape[:-1] + (n,), w_in.dtype)
    pe_slot = lambda w: jnp.concatenate([z(MLA_NOPE), w, z(MLA_HP - MLA_NOPE - MLA_ROPE)], axis=-1)
    cols = [mq, mkv, pe_slot(mpe), pe_slot(_rot_half_cols(mpe)), nq, nk, nv, dn, ab, z(LANES - 4 * DN_H)]
    return jnp.concatenate(cols, axis=-1).astype(BF16)


def _layout_w_q(w_q_up):
    w = w_q_up.reshape(DEPTH, MLA_Q_LORA, MLA_H, MLA_NOPE + MLA_ROPE)
    nope, rope = w[..., :MLA_NOPE], w[..., MLA_NOPE:]
    z = lambda n: jnp.zeros(w.shape[:-1] + (n,), w.dtype)
    a = jnp.concatenate([nope, rope, z(MLA_HP - MLA_NOPE - MLA_ROPE)], axis=-1)
    b = jnp.concatenate([z(MLA_NOPE), _rot_half_cols(rope), z(MLA_HP - MLA_NOPE - MLA_ROPE)], axis=-1)
    return jnp.concatenate([a.reshape(DEPTH, MLA_Q_LORA, MLA_W), b.reshape(DEPTH, MLA_Q_LORA, MLA_W)],
                           axis=-1).astype(BF16)


def _layout_w_kv(w_kv_up):
    w = w_kv_up.reshape(DEPTH, MLA_KV_LORA, MLA_H, MLA_NOPE + MLA_V)
    k = jnp.concatenate([w[..., :MLA_NOPE], jnp.zeros(w.shape[:-1] + (MLA_HP - MLA_NOPE,), w.dtype)], axis=-1)
    v = w[..., MLA_NOPE:]
    return jnp.concatenate([k.reshape(DEPTH, MLA_KV_LORA, MLA_W), v.reshape(DEPTH, MLA_KV_LORA, MLA_H * MLA_V)],
                           axis=-1).astype(BF16)


def _rope_tables(n_lat):
    t = jnp.arange(n_lat)
    pos = jnp.stack([t // GRID_W, t % GRID_W], axis=-1).astype(F32)
    n_freq = MLA_ROPE // 4
    inv = jnp.power(ROPE_THETA, -jnp.arange(n_freq, dtype=F32) / n_freq)
    ang = pos[:, :, None] * inv
    cos, sin = jnp.cos(ang), jnp.sin(ang)
    c32 = jnp.concatenate([cos[:, 0], cos[:, 0], cos[:, 1], cos[:, 1]], axis=-1)
    s32 = jnp.concatenate([sin[:, 0], sin[:, 0], sin[:, 1], sin[:, 1]], axis=-1)
    c32 = jnp.concatenate([jnp.ones((CTX_LEN, MLA_ROPE), F32), c32], axis=0)
    s32 = jnp.concatenate([jnp.zeros((CTX_LEN, MLA_ROPE), F32), s32], axis=0)
    S = CTX_LEN + n_lat
    ones = jnp.ones((S, MLA_NOPE), F32)
    z_nope = jnp.zeros((S, MLA_NOPE), F32)
    z_tail = jnp.zeros((S, MLA_HP - MLA_NOPE - MLA_ROPE), F32)
    scale = (MLA_NOPE + MLA_ROPE) ** -0.5
    tqc = jnp.concatenate([ones, c32, z_tail], axis=-1) * scale
    tqs = jnp.concatenate([z_nope, s32, z_tail], axis=-1) * scale
    tkc = jnp.concatenate([z_nope, c32, z_tail], axis=-1)
    tks = jnp.concatenate([z_nope, s32, z_tail], axis=-1)
    return tqc, tqs, tkc, tks


def kernel(x, c, ctx, c_ctx, w_ada, b_ada, g_mix, w_in, mla_g_q, mla_g_kv, mla_w_q_up, mla_w_kv_up, na_rel_bias,
           dn_conv_w, dn_a_log, dn_dt_bias, dn_g_out, w_out, g_ffn, w_gate, w_up, w_down, g_final):
    B, T, _ = x.shape
    assert B <= CTX_ROW and T == 32 * GRID_W and ctx.shape[1] == CTX_LEN
    S = CTX_LEN + T
    n_chunks = S // DN_CHUNK

    mod4 = _ada_call(_cond_rows(c, c_ctx), w_ada, b_ada).reshape(DEPTH, 16, 1, N_MOD * D_MODEL)

    w_in_p = _layout_w_in(w_in)
    wq_p = _layout_w_q(mla_w_q_up)
    wkv_p = _layout_w_kv(mla_w_kv_up)
    w_out_b = w_out.astype(BF16)
    w_gate_b = w_gate.astype(BF16)
    w_up_b = w_up.astype(BF16)
    w_down_b = w_down.astype(BF16)
    tabs = _rope_tables(T)
    na_bias = _na_bias_tables(na_rel_bias)
    gate_params = jnp.concatenate([dn_a_log.reshape(DEPTH, 2 * DN_H), dn_dt_bias.reshape(DEPTH, 2 * DN_H)],
                                  axis=-1).reshape(-1).astype(F32)
    vec = lambda g: g.reshape(DEPTH, 1, g.shape[-1])

    xs = jnp.concatenate([ctx, x], axis=1)
    for l in range(DEPTH):
        q, k, v, na_qkv, dn, ab = _in_call(l, xs, mod4, vec(g_mix), w_in_p, vec(mla_g_q), vec(mla_g_kv),
                                           wq_p, wkv_p, tabs)
        mla_o = _mla_call(q, k, v)
        na_o = _na_call(l, na_qkv, na_bias)
        ab_rows = ab[:, :, :4 * DN_H].reshape(B, n_chunks, DN_CHUNK, 4, DN_H).transpose(0, 4, 1, 3, 2)
        dn_o = _dn_call(l, dn, ab_rows, gate_params, dn_conv_w, vec(dn_g_out))
        xs, h2 = _out_call(l, xs, mla_o, na_o, dn_o, w_out_b, mod4, vec(g_ffn))
        xs = _ffn_call(l, xs, h2, w_gate_b, w_up_b, w_down_b, mod4)
    return _final_call(xs, g_final)
```
